```python
import jax, jax.numpy as jnp
from jax import lax
import numpy as np

D_MODEL = 1024
BATCH = 16
SEQ = 4096
DEPTH = 1
DEC_BATCH = 8
DEC_SEQ = 16
PAST_LEN = 1024

CHUNK = 64
D_RNN = D_MODEL
RNN_HEADS = 16
RNN_HEAD_DIM = D_RNN // RNN_HEADS
CONV_WIDTH = 4
RG_C = 8.0
D_SGU = D_MODEL
SGU_HEADS = 8
SGU_HEAD_DIM = D_SGU // SGU_HEADS
SGU_CHUNK = 128
D_FF = 2816
D_IN = 2 * D_RNN + 2 * D_SGU + 2 * D_MODEL
N_MOD = 9
EPS = 1e-6

kernel_name = 'hybrid_rglru_sgu_stream_step'


def _rms(x, g):
    xf = x.astype(jnp.float32)
    y = xf * lax.rsqrt(jnp.mean(xf * xf, axis=-1, keepdims=True) + EPS)
    return (y * g.astype(jnp.float32)).astype(x.dtype)


def _layernorm(x, g, b):
    xf = x.astype(jnp.float32)
    mu = jnp.mean(xf, axis=-1, keepdims=True)
    var = jnp.mean(jnp.square(xf - mu), axis=-1, keepdims=True)
    y = (xf - mu) * lax.rsqrt(var + EPS)
    return (y * g.astype(jnp.float32) + b.astype(jnp.float32)).astype(x.dtype)


def _modulate(xn, shift, scale):
    return xn * (1.0 + scale[:, None, :]) + shift[:, None, :]


def _swiglu(x, w_gate, w_up, w_down):
    return (jax.nn.silu(x @ w_gate) * (x @ w_up)) @ w_down


def _rg_lru(xc, h0, w_a, b_a, w_x, b_x, lam):
    B, T, _ = xc.shape
    xh = xc.reshape(B, T, RNN_HEADS, RNN_HEAD_DIM)
    r = jax.nn.sigmoid(jnp.einsum('bthi,hij->bthj', xh, w_a).reshape(B, T, D_RNN) + b_a)
    i = jax.nn.sigmoid(jnp.einsum('bthi,hij->bthj', xh, w_x).reshape(B, T, D_RNN) + b_x)
    log_a = -RG_C * r.astype(jnp.float32) * jax.nn.softplus(-lam.astype(jnp.float32))
    a = jnp.exp(log_a)
    mult = jnp.sqrt(-jnp.expm1(2.0 * log_a))
    b = mult * (i * xc).astype(jnp.float32)
    b = b.at[:, 0].add(a[:, 0] * h0.astype(jnp.float32))

    def combine(left, right):
        a1, b1 = left
        a2, b2 = right
        return a1 * a2, a2 * b1 + b2

    _, h = lax.associative_scan(combine, (a, b), axis=1)
    return h, h[:, -1]


def _sgu(u, v, g_v, b_v, w_s, b_s):
    B, T, _ = v.shape
    vn = _layernorm(v, g_v, b_v)
    n = -(-T // SGU_CHUNK)
    pad = n * SGU_CHUNK - T
    vp = jnp.pad(vn, ((0, 0), (0, pad), (0, 0))).reshape(B, n, SGU_CHUNK, SGU_HEADS, SGU_HEAD_DIM)
    mask = jnp.tril(jnp.ones((SGU_CHUNK, SGU_CHUNK), dtype=bool))
    w = jnp.where(mask[None], w_s, 0.0).astype(vp.dtype)
    s = jnp.einsum('hts,bnshc->bnthc', w, vp)
    s = s + jnp.transpose(b_s)[None, None, :, :, None].astype(vp.dtype)
    s = s.reshape(B, n * SGU_CHUNK, D_SGU)[:, :T]
    v_state = vn[:, (n - 1) * SGU_CHUNK:]
    return u * s, v_state


def _mixer(xn, conv_buf, h0, w_in, w_conv, b_conv, w_rg_a, b_rg_a, w_rg_x, b_rg_x,
           rg_lambda, ln_v_g, ln_v_b, w_spatial, b_spatial, w_out):
    T = xn.shape[1]
    z = xn @ w_in
    x_r = z[..., :D_RNN]
    gate_r = z[..., D_RNN:2 * D_RNN]
    uv = z[..., 2 * D_RNN:2 * D_RNN + 2 * D_SGU]
    g_a = z[..., 2 * D_RNN + 2 * D_SGU:2 * D_RNN + 2 * D_SGU + D_MODEL]
    g_b = z[..., 2 * D_RNN + 2 * D_SGU + D_MODEL:]
    xp = jnp.concatenate([conv_buf.astype(x_r.dtype), x_r], axis=1)
    xc = b_conv + sum(xp[:, k:k + T] * w_conv[k] for k in range(CONV_WIDTH))
    new_buf = xp[:, -(CONV_WIDTH - 1):]
    h, h_last = _rg_lru(xc, h0, w_rg_a, b_rg_a, w_rg_x, b_rg_x, rg_lambda)
    y_a = h.astype(xn.dtype) * jax.nn.gelu(gate_r)
    uv = jax.nn.gelu(uv)
    u, v = uv[..., :D_SGU], uv[..., D_SGU:]
    y_b, v_state = _sgu(u, v, ln_v_g, ln_v_b, w_spatial, b_spatial)
    merged = jax.nn.sigmoid(g_a) * y_a + jax.nn.sigmoid(g_b) * y_b
    return merged @ w_out, new_buf, h_last, v_state


def _layer(x, c, conv_buf, h0, w_ada, b_ada, g_ffn1, w_ffn1_gate, w_ffn1_up, w_ffn1_down,
           g_mix, w_in, w_conv, b_conv, w_rg_a, b_rg_a, w_rg_x, b_rg_x, rg_lambda,
           ln_v_g, ln_v_b, w_spatial, b_spatial, w_out,
           g_ffn2, w_ffn2_gate, w_ffn2_up, w_ffn2_down):
    B = x.shape[0]
    ada = (jax.nn.silu(c) @ w_ada + b_ada).reshape(B, N_MOD, D_MODEL)
    xn = _modulate(_rms(x, g_ffn1), ada[:, 0], ada[:, 1])
    x = x + 0.5 * ada[:, 2][:, None] * _swiglu(xn, w_ffn1_gate, w_ffn1_up, w_ffn1_down)
    xn = _modulate(_rms(x, g_mix), ada[:, 3], ada[:, 4])
    m, new_buf, h_last, v_state = _mixer(xn, conv_buf, h0, w_in, w_conv, b_conv, w_rg_a, b_rg_a,
                                         w_rg_x, b_rg_x, rg_lambda, ln_v_g, ln_v_b,
                                         w_spatial, b_spatial, w_out)
    x = x + ada[:, 5][:, None] * m
    xn = _modulate(_rms(x, g_ffn2), ada[:, 6], ada[:, 7])
    x = x + 0.5 * ada[:, 8][:, None] * _swiglu(xn, w_ffn2_gate, w_ffn2_up, w_ffn2_down)
    return x, new_buf, h_last, v_state


def _final(x, c, w_ada_final, b_ada_final, g_final):
    B = x.shape[0]
    ada = (jax.nn.silu(c) @ w_ada_final + b_ada_final).reshape(B, 2, D_MODEL)
    return _modulate(_rms(x, g_final), ada[:, 0], ada[:, 1])


def setup_inputs(seed: int = 0) -> dict:
    key = jax.random.key(seed)
    ks = iter(jax.random.split(key, 48))

    def nrm(shape, scale):
        return jax.random.normal(next(ks), shape, jnp.float32) * scale

    L, D = DEPTH, D_MODEL
    inp = {}
    inp['x_prompt'] = nrm((BATCH, SEQ, D), 1.0)
    inp['x_sample'] = nrm((DEC_BATCH, DEC_SEQ, D), 1.0)
    inp['cache_conv'] = nrm((L, DEC_BATCH, CONV_WIDTH - 1, D_RNN), 1.0)
    inp['state_rglru'] = nrm((L, DEC_BATCH, D_RNN), 0.5)
    inp['c_prompt'] = nrm((BATCH, D), 1.0)
    inp['c_sample'] = nrm((DEC_BATCH, D), 1.0)
    inp['w_ada_final'] = nrm((D, 2 * D), 0.5 * D ** -0.5)
    inp['b_ada_final'] = nrm((2 * D,), 0.01)
    inp['g_final'] = 1.0 + nrm((D,), 0.02)
    inp['w_ada'] = nrm((L, D, N_MOD * D), 0.5 * D ** -0.5)
    inp['b_ada'] = nrm((L, N_MOD * D), 0.01)
    inp['g_ffn1'] = 1.0 + nrm((L, D), 0.02)
    inp['w_ffn1_gate'] = nrm((L, D, D_FF), D ** -0.5)
    inp['w_ffn1_up'] = nrm((L, D, D_FF), D ** -0.5)
    inp['w_ffn1_down'] = nrm((L, D_FF, D), D_FF ** -0.5)
    inp['g_mix'] = 1.0 + nrm((L, D), 0.02)
    inp['w_in'] = nrm((L, D, D_IN), D ** -0.5)
    inp['w_conv'] = nrm((L, CONV_WIDTH, D_RNN), CONV_WIDTH ** -0.5)
    inp['b_conv'] = nrm((L, D_RNN), 0.01)
    inp['w_rg_a'] = nrm((L, RNN_HEADS, RNN_HEAD_DIM, RNN_HEAD_DIM), RNN_HEAD_DIM ** -0.5)
    inp['b_rg_a'] = nrm((L, D_RNN), 0.01)
    inp['w_rg_x'] = nrm((L, RNN_HEADS, RNN_HEAD_DIM, RNN_HEAD_DIM), RNN_HEAD_DIM ** -0.5)
    inp['b_rg_x'] = nrm((L, D_RNN), 0.01)
    a_c = jax.random.uniform(next(ks), (L, D_RNN), jnp.float32, minval=0.9, maxval=0.999)
    s = a_c ** (1.0 / RG_C)
    inp['rg_lambda'] = jnp.log(s) - jnp.log1p(-s)
    inp['ln_v_g'] = 1.0 + nrm((L, D_SGU), 0.02)
    inp['ln_v_b'] = nrm((L, D_SGU), 0.01)
    inp['w_spatial'] = nrm((L, SGU_HEADS, SGU_CHUNK, SGU_CHUNK), SGU_CHUNK ** -0.5)
    inp['b_spatial'] = 1.0 + nrm((L, SGU_HEADS, SGU_CHUNK), 0.02)
    inp['w_out'] = nrm((L, D, D), D ** -0.5)
    inp['g_ffn2'] = 1.0 + nrm((L, D), 0.02)
    inp['w_ffn2_gate'] = nrm((L, D, D_FF), D ** -0.5)
    inp['w_ffn2_up'] = nrm((L, D, D_FF), D ** -0.5)
    inp['w_ffn2_down'] = nrm((L, D_FF, D), D_FF ** -0.5)
    return inp


def reference(x_prompt, x_sample, cache_conv, state_rglru, c_prompt, c_sample,
              w_ada_final, b_ada_final, g_final, w_ada, b_ada, g_ffn1, w_ffn1_gate,
              w_ffn1_up, w_ffn1_down, g_mix, w_in, w_conv, b_conv, w_rg_a, b_rg_a,
              w_rg_x, b_rg_x, rg_lambda, ln_v_g, ln_v_b, w_spatial, b_spatial, w_out,
              g_ffn2, w_ffn2_gate, w_ffn2_up, w_ffn2_down):
    hp, hs = x_prompt, x_sample
    conv_p, rg_p, v_p, conv_s, rg_s, v_s = [], [], [], [], [], []
    for l in range(DEPTH):
        lw = (w_ada[l], b_ada[l], g_ffn1[l], w_ffn1_gate[l], w_ffn1_up[l], w_ffn1_down[l],
              g_mix[l], w_in[l], w_conv[l], b_conv[l], w_rg_a[l], b_rg_a[l], w_rg_x[l],
              b_rg_x[l], rg_lambda[l], ln_v_g[l], ln_v_b[l], w_spatial[l], b_spatial[l],
              w_out[l], g_ffn2[l], w_ffn2_gate[l], w_ffn2_up[l], w_ffn2_down[l])
        zero_buf = jnp.zeros((hp.shape[0], CONV_WIDTH - 1, D_RNN), hp.dtype)
        zero_h = jnp.zeros((hp.shape[0], D_RNN), jnp.float32)
        hp, cb, hl, vs = _layer(hp, c_prompt, zero_buf, zero_h, *lw)
        conv_p.append(cb); rg_p.append(hl); v_p.append(vs)
        hs, cb, hl, vs = _layer(hs, c_sample, cache_conv[l], state_rglru[l], *lw)
        conv_s.append(cb); rg_s.append(hl); v_s.append(vs)
    y_prompt = _final(hp, c_prompt, w_ada_final, b_ada_final, g_final)
    y_sample = _final(hs, c_sample, w_ada_final, b_ada_final, g_final)
    return (y_prompt, y_sample, jnp.stack(conv_p), jnp.stack(rg_p), jnp.stack(v_p),
            jnp.stack(conv_s), jnp.stack(rg_s), jnp.stack(v_s))
```

```python
import functools

import jax
import jax.numpy as jnp
from jax import lax
from jax.experimental import pallas as pl
from jax.experimental.pallas import tpu as pltpu

F32 = jnp.float32
BF16 = jnp.bfloat16

D_MODEL = 1024
D_FF = 2816
CONV_WIDTH = 4
RNN_HEADS = 16
RNN_HEAD_DIM = D_MODEL // RNN_HEADS
RG_C = 8.0
SGU_HEADS = 8
SGU_HEAD_DIM = D_MODEL // SGU_HEADS
SGU_CHUNK = 128
N_MOD = 9
EPS = 1e-6

SUBLANES = 8
FF_COLS = 256
RG_GROUP = 256
N_RG_GROUPS = D_MODEL // RG_GROUP
CONV_PAD = SUBLANES
ADA_ROWS = 32
ADA_COLS = 1024
VMEM_LIMIT_BYTES = 56 * 1024 * 1024


def _dot(a, b):
    return jnp.dot(a, b, preferred_element_type=F32)


def _rms_mod(x, g, shift, scale):
    ms = jnp.mean(x * x, axis=-1, keepdims=True)
    y = (x * lax.rsqrt(ms + EPS)) * g
    return y * (1.0 + scale) + shift


def _gelu_tanh(x):
    c = 0.7978845608028654
    return 0.5 * x * (1.0 + jnp.tanh(c * (x + 0.044715 * (x * x * x))))


def _const_spec(shape):
    nd = len(shape)
    return pl.BlockSpec(shape, lambda *_: (0,) * nd, pipeline_mode=pl.Buffered(1))


def _ada_kernel(c_ref, w_ref, b_ref, o_ref):
    c = c_ref[...]
    sc = (c * jax.nn.sigmoid(c)).astype(BF16)
    o_ref[...] = _dot(sc, w_ref[...].astype(BF16)) + b_ref[...]


def _ada_call(c_rows, w, b):
    n = w.shape[1]
    return pl.pallas_call(
        _ada_kernel,
        grid=(n // ADA_COLS,),
        in_specs=[
            pl.BlockSpec((ADA_ROWS, D_MODEL), lambda j: (0, 0)),
            pl.BlockSpec((D_MODEL, ADA_COLS), lambda j: (0, j)),
            pl.BlockSpec((1, ADA_COLS), lambda j: (0, j)),
        ],
        out_specs=pl.BlockSpec((ADA_ROWS, ADA_COLS), lambda j: (0, j)),
        out_shape=jax.ShapeDtypeStruct((ADA_ROWS, n), F32),
        compiler_params=pltpu.CompilerParams(dimension_semantics=("parallel",)),
        name="ada_proj",
    )(c_rows, w, b.reshape(1, n))


def _ffn_kernel(x_ref, mod_ref, g_ref, wg_ref, wu_ref, wd_ref, *rest, final):
    if final:
        modf_ref, gf_ref, o_ref = rest
    else:
        (o_ref,) = rest
    x = x_ref[...]
    xb = _rms_mod(x, g_ref[...], mod_ref[0], mod_ref[1]).astype(BF16)
    acc = jnp.zeros(x.shape, F32)
    for c in range(D_FF // FF_COLS):
        cs = slice(c * FF_COLS, (c + 1) * FF_COLS)
        g = _dot(xb, wg_ref[:, cs])
        u = _dot(xb, wu_ref[:, cs])
        a = ((g * jax.nn.sigmoid(g)) * u).astype(BF16)
        acc = acc + _dot(a, wd_ref[cs, :])
    y = x + (0.5 * mod_ref[2]) * acc
    if final:
        y = _rms_mod(y, gf_ref[...], modf_ref[0], modf_ref[1])
    o_ref[...] = y


def _ffn_call(x, mod, g, wg, wu, wd, tm, modf=None, gf=None):
    b, t, _ = x.shape
    r = mod.shape[2]
    final = modf is not None
    in_specs = [
        pl.BlockSpec((None, tm, D_MODEL), lambda i, j: (i, j, 0)),
        pl.BlockSpec((None, 3, r, D_MODEL), lambda i, j: (i, 0, 0, 0)),
        _const_spec((1, D_MODEL)),
        _const_spec((D_MODEL, D_FF)),
        _const_spec((D_MODEL, D_FF)),
        _const_spec((D_FF, D_MODEL)),
    ]
    args = [x, mod, g, wg, wu, wd]
    if final:
        in_specs += [
            pl.BlockSpec((None, 2, r, D_MODEL), lambda i, j: (i, 0, 0, 0)),
            _const_spec((1, D_MODEL)),
        ]
        args += [modf, gf]
    return pl.pallas_call(
        functools.partial(_ffn_kernel, final=final),
        grid=(b, t // tm),
        in_specs=in_specs,
        out_specs=pl.BlockSpec((None, tm, D_MODEL), lambda i, j: (i, j, 0)),
        out_shape=jax.ShapeDtypeStruct((b, t, D_MODEL), F32),
        compiler_params=pltpu.CompilerParams(
            dimension_semantics=("parallel", "parallel"),
            vmem_limit_bytes=VMEM_LIMIT_BYTES,
        ),
        name="ffn_final" if final else "ffn",
    )(*args)


def _scan_rows(a, b):
    row = lax.broadcasted_iota(jnp.int32, a.shape, 0) % SUBLANES
    for s in (1, 2, 4):
        keep = row >= s
        a_prev = jnp.where(keep, pltpu.roll(a, s, 0), 1.0)
        b_prev = jnp.where(keep, pltpu.roll(b, s, 0), 0.0)
        b = b + a * b_prev
        a = a * a_prev
    return a, b


def _mixer_kernel(x_ref, mod_ref, conv0_ref, h0_ref, g_ref, win_ref, wconv_ref, bconv_ref,
                  wrg_ref, brg_ref, lam_ref, lng_ref, lnb_ref, wsp_ref, bsp_ref, wout_ref,
                  o_ref, conv_o_ref, h_o_ref, v_o_ref,
                  xr_buf, a_buf, b_buf, h_buf, h_state, wtril, *, tm, ch):
    bi = pl.program_id(0)
    ti = pl.program_id(1)
    n_t = pl.num_programs(1)

    @pl.when((bi == 0) & (ti == 0))
    def _():
        row = lax.broadcasted_iota(jnp.int32, (SGU_CHUNK, SGU_CHUNK), 0)
        col = lax.broadcasted_iota(jnp.int32, (SGU_CHUNK, SGU_CHUNK), 1)
        for hh in range(SGU_HEADS):
            wtril[hh] = jnp.where(col <= row, wsp_ref[hh], 0.0).astype(BF16)

    @pl.when(ti == 0)
    def _():
        xr_buf[0:CONV_PAD, :] = jnp.zeros((CONV_PAD, D_MODEL), F32)
        xr_buf[CONV_PAD - (CONV_WIDTH - 1):CONV_PAD, :] = conv0_ref[...]
        h_state[...] = h0_ref[...]

    x = x_ref[...]
    xb = _rms_mod(x, g_ref[...], mod_ref[0], mod_ref[1]).astype(BF16)

    xr_buf[CONV_PAD:CONV_PAD + tm, :] = _dot(xb, win_ref[:, 0:D_MODEL])
    xc = bconv_ref[...] + xr_buf[CONV_PAD:CONV_PAD + tm, :] * wconv_ref[CONV_WIDTH - 1:CONV_WIDTH, :]
    for k in range(CONV_WIDTH - 1):
        off = CONV_PAD - (CONV_WIDTH - 1) + k
        xc = xc + xr_buf[off:off + tm, :] * wconv_ref[k:k + 1, :]
    tail = xr_buf[tm:tm + CONV_PAD, :]
    conv_o_ref[...] = xr_buf[tm + CONV_PAD - (CONV_WIDTH - 1):tm + CONV_PAD, :]
    xr_buf[0:CONV_PAD, :] = tail

    xcb = xc.astype(BF16)
    lam = lam_ref[...]
    neg_sp = -RG_C * (jnp.maximum(-lam, 0.0) + jnp.log1p(jnp.exp(-jnp.abs(lam))))
    for g in range(N_RG_GROUPS):
        cs = slice(g * RG_GROUP, (g + 1) * RG_GROUP)
        ri = _dot(xcb[:, cs], wrg_ref[g])
        r = jax.nn.sigmoid(ri[:, 0:RG_GROUP] + brg_ref[0:1, cs])
        i = jax.nn.sigmoid(ri[:, RG_GROUP:2 * RG_GROUP] + brg_ref[1:2, cs])
        log_a = r * neg_sp[:, cs]
        a = jnp.exp(log_a)
        th = jnp.tanh(log_a)
        mult = jnp.sqrt(-2.0 * th / (1.0 - th))
        bb = mult * (i * xc[:, cs])
        a_s, b_s = _scan_rows(a, bb)
        a_buf[:, cs] = a_s
        b_buf[:, cs] = b_s

    h = h_state[...]
    for q in range(tm // SUBLANES):
        rs = slice(q * SUBLANES, (q + 1) * SUBLANES)
        hq = a_buf[rs, :] * h + b_buf[rs, :]
        h_buf[rs, :] = hq
        h = hq[SUBLANES - 1:SUBLANES, :]
    h_state[...] = h
    h_o_ref[...] = h

    gate_r = _dot(xb, win_ref[:, D_MODEL:2 * D_MODEL])
    y_a = h_buf[...] * _gelu_tanh(gate_r)

    u = _gelu_tanh(_dot(xb, win_ref[:, 2 * D_MODEL:3 * D_MODEL]))
    v = _gelu_tanh(_dot(xb, win_ref[:, 3 * D_MODEL:4 * D_MODEL]))
    mu = jnp.mean(v, axis=-1, keepdims=True)
    vc = v - mu
    var = jnp.mean(vc * vc, axis=-1, keepdims=True)
    vn = (vc * lax.rsqrt(var + EPS)) * lng_ref[...] + lnb_ref[...]

    @pl.when(ti == n_t - 1)
    def _():
        v_o_ref[...] = vn[tm - ch:tm, :]

    vnb = vn.astype(BF16)
    s_chunks = []
    for j in range(tm // ch):
        vj = vnb[j * ch:(j + 1) * ch, :]
        if ch < SGU_CHUNK:
            vj = jnp.concatenate([vj, jnp.zeros((SGU_CHUNK - ch, D_MODEL), BF16)], axis=0)
        heads = []
        for hh in range(SGU_HEADS):
            hs = slice(hh * SGU_HEAD_DIM, (hh + 1) * SGU_HEAD_DIM)
            heads.append(_dot(wtril[hh, 0:ch, :], vj[:, hs]))
        s_chunks.append(jnp.concatenate(heads, axis=1) + bsp_ref[0:ch, :])
    s = s_chunks[0] if len(s_chunks) == 1 else jnp.concatenate(s_chunks, axis=0)
    y_b = u * s

    g_a = _dot(xb, win_ref[:, 4 * D_MODEL:5 * D_MODEL])
    g_b = _dot(xb, win_ref[:, 5 * D_MODEL:6 * D_MODEL])
    merged = (jax.nn.sigmoid(g_a) * y_a + jax.nn.sigmoid(g_b) * y_b).astype(BF16)
    o_ref[...] = x + mod_ref[2] * _dot(merged, wout_ref[...])


def _mixer_call(x, mod, conv0, h0, g, win, wconv, bconv, wrg, brg, lam, lng, lnb, wsp, bsp, wout, tm):
    b, t, _ = x.shape
    ch = min(SGU_CHUNK, t)
    kw = CONV_WIDTH - 1
    return pl.pallas_call(
        functools.partial(_mixer_kernel, tm=tm, ch=ch),
        grid=(b, t // tm),
        in_specs=[
            pl.BlockSpec((None, tm, D_MODEL), lambda i, j: (i, j, 0)),
            pl.BlockSpec((None, 3, 1, D_MODEL), lambda i, j: (i, 0, 0, 0)),
            pl.BlockSpec((None, kw, D_MODEL), lambda i, j: (i, 0, 0)),
            pl.BlockSpec((None, 1, D_MODEL), lambda i, j: (i, 0, 0)),
            _const_spec((1, D_MODEL)),
            _const_spec((D_MODEL, 6 * D_MODEL)),
            _const_spec((CONV_WIDTH, D_MODEL)),
            _const_spec((1, D_MODEL)),
            _const_spec((N_RG_GROUPS, RG_GROUP, 2 * RG_GROUP)),
            _const_spec((2, D_MODEL)),
            _const_spec((1, D_MODEL)),
            _const_spec((1, D_MODEL)),
            _const_spec((1, D_MODEL)),
            _const_spec((SGU_HEADS, SGU_CHUNK, SGU_CHUNK)),
            _const_spec((SGU_CHUNK, D_MODEL)),
            _const_spec((D_MODEL, D_MODEL)),
        ],
        out_specs=[
            pl.BlockSpec((None, tm, D_MODEL), lambda i, j: (i, j, 0)),
            pl.BlockSpec((None, kw, D_MODEL), lambda i, j: (i, 0, 0)),
            pl.BlockSpec((None, 1, D_MODEL), lambda i, j: (i, 0, 0)),
            pl.BlockSpec((None, ch, D_MODEL), lambda i, j: (i, 0, 0)),
        ],
        out_shape=[
            jax.ShapeDtypeStruct((b, t, D_MODEL), F32),
            jax.ShapeDtypeStruct((b, kw, D_MODEL), F32),
            jax.ShapeDtypeStruct((b, 1, D_MODEL), F32),
            jax.ShapeDtypeStruct((b, ch, D_MODEL), F32),
        ],
        scratch_shapes=[
            pltpu.VMEM((tm + CONV_PAD, D_MODEL), F32),
            pltpu.VMEM((tm, D_MODEL), F32),
            pltpu.VMEM((tm, D_MODEL), F32),
            pltpu.VMEM((tm, D_MODEL), F32),
            pltpu.VMEM((1, D_MODEL), F32),
            pltpu.VMEM((SGU_HEADS, SGU_CHUNK, SGU_CHUNK), BF16),
        ],
        compiler_params=pltpu.CompilerParams(
            dimension_semantics=("arbitrary", "arbitrary"),
            vmem_limit_bytes=VMEM_LIMIT_BYTES,
        ),
        name="mixer",
    )(x, mod, conv0, h0, g, win, wconv, bconv, wrg, brg, lam, lng, lnb, wsp, bsp, wout)


def _block_diag_gates(w_a, w_x):
    hpg = RG_GROUP // RNN_HEAD_DIM
    eye = jnp.eye(hpg, dtype=w_a.dtype)

    def bd(w):
        w = w.reshape(N_RG_GROUPS, hpg, RNN_HEAD_DIM, RNN_HEAD_DIM)
        full = w[:, :, :, None, :] * eye[None, :, None, :, None]
        return full.reshape(N_RG_GROUPS, RG_GROUP, RG_GROUP)

    return jnp.concatenate([bd(w_a), bd(w_x)], axis=-1).astype(BF16)


def _per_row(mod, seq):
    b, k, d = mod.shape
    return jnp.broadcast_to(mod.transpose(1, 0, 2)[:, :, None, :], (k, b, seq, d)).reshape(1, k, b * seq, d)


def kernel(x_prompt, x_sample, cache_conv, state_rglru, c_prompt, c_sample, w_ada_final, b_ada_final, g_final, w_ada, b_ada, g_ffn1, w_ffn1_gate, w_ffn1_up, w_ffn1_down, g_mix, w_in, w_conv, b_conv, w_rg_a, b_rg_a, w_rg_x, b_rg_x, rg_lambda, ln_v_g, ln_v_b, w_spatial, b_spatial, w_out, g_ffn2, w_ffn2_gate, w_ffn2_up, w_ffn2_down):
    bp, tp, _ = x_prompt.shape
    bs, ts, _ = x_sample.shape
    assert w_ada.shape[0] == 1, "one trunk layer"

    c_rows = jnp.concatenate(
        [c_prompt, c_sample, jnp.zeros((ADA_ROWS - bp - bs, D_MODEL), F32)], axis=0)
    ada = _ada_call(c_rows, w_ada[0], b_ada[0]).reshape(ADA_ROWS, N_MOD, D_MODEL)
    ada_f = _ada_call(c_rows, w_ada_final, b_ada_final).reshape(ADA_ROWS, 2, D_MODEL)
    ada_p, ada_s = ada[:bp], ada[bp:bp + bs]
    adaf_p, adaf_s = ada_f[:bp], ada_f[bp:bp + bs]

    row = lambda v: v.reshape(1, D_MODEL)
    ffn1_w = (row(g_ffn1[0]), w_ffn1_gate[0].astype(BF16), w_ffn1_up[0].astype(BF16), w_ffn1_down[0].astype(BF16))
    ffn2_w = (row(g_ffn2[0]), w_ffn2_gate[0].astype(BF16), w_ffn2_up[0].astype(BF16), w_ffn2_down[0].astype(BF16))
    bias_rows = jnp.repeat(b_spatial[0].T, SGU_HEAD_DIM, axis=1)
    mix_w = (row(g_mix[0]), w_in[0].astype(BF16), w_conv[0], row(b_conv[0]),
             _block_diag_gates(w_rg_a[0], w_rg_x[0]), jnp.stack([b_rg_a[0], b_rg_x[0]]),
             row(rg_lambda[0]), row(ln_v_g[0]), row(ln_v_b[0]), w_spatial[0], bias_rows,
             w_out[0].astype(BF16))

    def layer(x, ada_b, adaf_b, conv0, h0, tm_ffn, tm_mix, flat):
        b, t, _ = x.shape
        if flat:
            xf = x.reshape(1, b * t, D_MODEL)
            m1, m3, mf = _per_row(ada_b[:, 0:3], t), _per_row(ada_b[:, 6:9], t), _per_row(adaf_b, t)
        else:
            xf = x
            m1, m3, mf = ada_b[:, 0:3, None, :], ada_b[:, 6:9, None, :], adaf_b[:, :, None, :]
        h = _ffn_call(xf, m1, *ffn1_w, tm=tm_ffn).reshape(b, t, D_MODEL)
        h, conv_n, h_n, v_n = _mixer_call(h, ada_b[:, 3:6, None, :], conv0, h0[:, None, :], *mix_w, tm=tm_mix)
        y = _ffn_call(h.reshape(xf.shape), m3, *ffn2_w, tm=tm_ffn, modf=mf, gf=row(g_final))
        return y.reshape(b, t, D_MODEL), conv_n[None], h_n[:, 0][None], v_n[None]

    yp, conv_p, rg_p, v_p = layer(
        x_prompt, ada_p, adaf_p, jnp.zeros((bp, CONV_WIDTH - 1, D_MODEL), F32), jnp.zeros((bp, D_MODEL), F32),
        tm_ffn=512, tm_mix=256, flat=False)
    ys, conv_s, rg_s, v_s = layer(
        x_sample, ada_s, adaf_s, cache_conv[0], state_rglru[0],
        tm_ffn=bs * ts, tm_mix=ts, flat=True)
    return yp, ys, conv_p, rg_p, v_p, conv_s, rg_s, v_s
```

```python
import functools

import jax
import jax.numpy as jnp
from jax import lax
from jax.experimental import pallas as pl
from jax.experimental.pallas import tpu as pltpu

F32 = jnp.float32
BF16 = jnp.bfloat16

D_MODEL = 1024
D_FF = 2816
CONV_WIDTH = 4
RNN_HEADS = 16
RNN_HEAD_DIM = D_MODEL // RNN_HEADS
RG_C = 8.0
SGU_HEADS = 8
SGU_HEAD_DIM = D_MODEL // SGU_HEADS
SGU_CHUNK = 128
N_MOD = 9
EPS = 1e-6

SUBLANES = 8
LANES = 128
N_COL_SLABS = D_MODEL // LANES
FF_COLS = 256
RG_GROUP = 256
N_RG_GROUPS = D_MODEL // RG_GROUP
CONV_PAD = SUBLANES
ADA_ROWS = 32
ADA_COLS = 1024
TM_FFN = 512
TM_MIX = 256
VMEM_LIMIT_BYTES = 56 * 1024 * 1024
GELU_K0 = 0.7978845608028654
GELU_K1 = GELU_K0 * 0.044715


def _dot(a, b):
    return jnp.dot(a, b, preferred_element_type=F32)


def _rms_mod(x, gain, shift):
    ms = jnp.mean(x * x, axis=-1, keepdims=True)
    return (x * lax.rsqrt(ms + EPS)) * gain + shift


def _gelu_x2(x):
    return x * (1.0 + jnp.tanh(x * (GELU_K0 + GELU_K1 * (x * x))))


def _sigmoid_x2(x):
    return 1.0 + jnp.tanh(0.5 * x)


def _const_spec(shape):
    nd = len(shape)
    return pl.BlockSpec(shape, lambda *_: (0,) * nd, pipeline_mode=pl.Buffered(1))


def _ada_kernel(c_ref, w_ref, b_ref, o_ref):
    c = c_ref[...]
    sc = (c * jax.nn.sigmoid(c)).astype(BF16)
    o_ref[...] = _dot(sc, w_ref[...].astype(BF16)) + b_ref[...]


def _ada_call(c_rows, w, b):
    n = w.shape[1]
    return pl.pallas_call(
        _ada_kernel,
        grid=(n // ADA_COLS,),
        in_specs=[
            pl.BlockSpec((ADA_ROWS, D_MODEL), lambda j: (0, 0)),
            pl.BlockSpec((D_MODEL, ADA_COLS), lambda j: (0, j)),
            pl.BlockSpec((1, ADA_COLS), lambda j: (0, j)),
        ],
        out_specs=pl.BlockSpec((ADA_ROWS, ADA_COLS), lambda j: (0, j)),
        out_shape=jax.ShapeDtypeStruct((ADA_ROWS, n), F32),
        compiler_params=pltpu.CompilerParams(dimension_semantics=("parallel",)),
        name="ada_proj",
    )(c_rows, w, b.reshape(1, n))


def _ffn_kernel(x_ref, mod_ref, g_ref, wg_ref, wu_ref, wd_ref, *rest, final):
    if final:
        modf_ref, gf_ref, o_ref = rest
    else:
        (o_ref,) = rest
    x = x_ref[...]
    xb = _rms_mod(x, g_ref[...] * (1.0 + mod_ref[1]), mod_ref[0]).astype(BF16)
    acc = jnp.zeros(x.shape, F32)
    for c in range(D_FF // FF_COLS):
        cs = slice(c * FF_COLS, (c + 1) * FF_COLS)
        g = _dot(xb, wg_ref[:, cs])
        u = _dot(xb, wu_ref[:, cs])
        a = ((g * _sigmoid_x2(g)) * u).astype(BF16)
        acc = acc + _dot(a, wd_ref[cs, 0:D_MODEL])
    y = x + (0.25 * mod_ref[2]) * acc
    if final:
        y = _rms_mod(y, gf_ref[...] * (1.0 + modf_ref[1]), modf_ref[0])
    o_ref[...] = y


def _ffn_call(x, mod, g, wg, wu, wd, tm, modf=None, gf=None):
    b, t, _ = x.shape
    r = mod.shape[2]
    final = modf is not None
    in_specs = [
        pl.BlockSpec((None, tm, D_MODEL), lambda i, j: (i, j, 0)),
        pl.BlockSpec((None, 3, r, D_MODEL), lambda i, j: (i, 0, 0, 0)),
        _const_spec((1, D_MODEL)),
        _const_spec(wg.shape),
        _const_spec(wu.shape),
        _const_spec(wd.shape),
    ]
    args = [x, mod, g, wg, wu, wd]
    if final:
        in_specs += [
            pl.BlockSpec((None, 2, r, D_MODEL), lambda i, j: (i, 0, 0, 0)),
            _const_spec((1, D_MODEL)),
        ]
        args += [modf, gf]
    return pl.pallas_call(
        functools.partial(_ffn_kernel, final=final),
        grid=(b, t // tm),
        in_specs=in_specs,
        out_specs=pl.BlockSpec((None, tm, D_MODEL), lambda i, j: (i, j, 0)),
        out_shape=jax.ShapeDtypeStruct((b, t, D_MODEL), F32),
        compiler_params=pltpu.CompilerParams(
            dimension_semantics=("parallel", "parallel"),
            vmem_limit_bytes=VMEM_LIMIT_BYTES,
        ),
        name="ffn_final" if final else "ffn",
    )(*args)


def _init_tril(wtril, wsp_ref):
    row = lax.broadcasted_iota(jnp.int32, (SGU_CHUNK, SGU_CHUNK), 0)
    col = lax.broadcasted_iota(jnp.int32, (SGU_CHUNK, SGU_CHUNK), 1)
    for hh in range(SGU_HEADS):
        wtril[hh] = jnp.where(col <= row, wsp_ref[hh], 0.0).astype(BF16)


def _neg_softplus_scaled(lam):
    return -RG_C * (jnp.maximum(-lam, 0.0) + jnp.log1p(jnp.exp(-jnp.abs(lam))))


def _rglru_terms(ri, xc, half_b, half_nsp):
    c = xc.shape[1]
    tr = jnp.tanh(0.5 * ri[:, 0:c] + half_b[0:1])
    ti = jnp.tanh(0.5 * ri[:, c:2 * c] + half_b[1:2])
    log_a = tr * half_nsp + half_nsp
    a = jnp.exp(log_a)
    th = jnp.tanh(log_a)
    mult = jnp.sqrt((-0.5 * th) / (1.0 - th))
    return a, mult * ((ti + 1.0) * xc)


def _layernorm(v, g, b):
    mu = jnp.mean(v, axis=-1, keepdims=True)
    vc = v - mu
    var = jnp.mean(vc * vc, axis=-1, keepdims=True)
    return (vc * lax.rsqrt(var + EPS)) * g + b


def _spatial_gate(vnb, wtril, bias_ref, tm, ch):
    chunks = []
    for j in range(tm // ch):
        vj = vnb[j * ch:(j + 1) * ch, :]
        if ch < SGU_CHUNK:
            vj = jnp.concatenate([vj, jnp.zeros((SGU_CHUNK - ch, D_MODEL), BF16)], axis=0)
        heads = []
        for hh in range(SGU_HEADS):
            hs = slice(hh * SGU_HEAD_DIM, (hh + 1) * SGU_HEAD_DIM)
            heads.append(_dot(wtril[hh, 0:ch, :], vj[:, hs]))
        chunks.append(jnp.concatenate(heads, axis=1) + bias_ref[0:ch, :])
    return chunks[0] if len(chunks) == 1 else jnp.concatenate(chunks, axis=0)


def _mixer_tail(x, z_gate, z_rest, h_rows, mod_ref, lng_ref, lnb_ref, wtril, bsp_ref, wout_ref, o_ref, v_o_ref,
                tm, ch):
    col = lambda k: z_rest[:, k * D_MODEL:(k + 1) * D_MODEL]
    ya = h_rows * _gelu_x2(z_gate)
    u = _gelu_x2(col(0))
    vn = _layernorm(0.5 * _gelu_x2(col(1)), lng_ref[...], lnb_ref[...])
    v_o_ref[...] = vn[tm - ch:tm, :]
    yb = u * _spatial_gate(vn.astype(BF16), wtril, bsp_ref, tm, ch)
    merged = (_sigmoid_x2(col(2)) * ya + _sigmoid_x2(col(3)) * yb).astype(BF16)
    o_ref[...] = x + (0.25 * mod_ref[2]) * _dot(merged, wout_ref[:, 0:D_MODEL])


def _carry_rows(a, h, c0):
    n_rows = a.shape[0]
    sub = lax.broadcasted_iota(jnp.int32, (SUBLANES, D_MODEL), 0)
    cb = jnp.broadcast_to(c0, (SUBLANES, D_MODEL))
    out = []
    for j in range(n_rows // SUBLANES):
        aj = a[j * SUBLANES:(j + 1) * SUBLANES, :]
        hj = h[j * SUBLANES:(j + 1) * SUBLANES, :]
        cin = cb
        for s in range(SUBLANES):
            if s:
                cin = jnp.where(sub == s, cb, cin)
            nxt = aj * cb + hj
            cb = jnp.broadcast_to(nxt[s:s + 1, :], (SUBLANES, D_MODEL))
        out.append(cin)
    return jnp.concatenate(out, axis=0), cb[0:1, :]


def _mixer_kernel(x_ref, mod_ref, conv0_ref, h0_ref, g_ref, win_ref, wconv_ref, bconv_ref,
                  wrg_ref, brg_ref, lam_ref, lng_ref, lnb_ref, wsp_ref, bsp_ref, wout_ref,
                  o_ref, conv_o_ref, h_o_ref, v_o_ref,
                  xr_slab, h_slab, a_buf, b_buf, h_state, wtril, *, tm):
    bi = pl.program_id(0)
    ti = pl.program_id(1)
    n_grp = tm // SUBLANES
    kw = CONV_WIDTH - 1

    @pl.when((bi == 0) & (ti == 0))
    def _():
        _init_tril(wtril, wsp_ref)

    @pl.when(ti == 0)
    def _():
        for c in range(N_COL_SLABS):
            xr_slab[c, 0:CONV_PAD, :] = jnp.zeros((CONV_PAD, LANES), F32)
            xr_slab[c, CONV_PAD - kw:CONV_PAD, :] = conv0_ref[:, c * LANES:(c + 1) * LANES]
        h_state[...] = h0_ref[...]

    x = x_ref[...]
    xb = _rms_mod(x, g_ref[...] * (1.0 + mod_ref[1]), mod_ref[0]).astype(BF16)

    z_rg = _dot(xb, win_ref[:, 0:2 * D_MODEL])
    for c in range(N_COL_SLABS):
        xr_slab[c, CONV_PAD:CONV_PAD + tm, :] = z_rg[:, c * LANES:(c + 1) * LANES]
        conv_o_ref[:, c * LANES:(c + 1) * LANES] = xr_slab[c, tm + CONV_PAD - kw:tm + CONV_PAD, :]

    conv_out, gate_pre = [], []
    for g in range(N_RG_GROUPS):
        cs = slice(g * RG_GROUP, (g + 1) * RG_GROUP)
        slabs = range(g * RG_GROUP // LANES, (g + 1) * RG_GROUP // LANES)

        def rows_from(start):
            return jnp.concatenate(
                [xr_slab[c, pl.ds(start, n_grp, stride=SUBLANES), :] for c in slabs], axis=1)

        shifted = {s: rows_from(s) for s in range(CONV_PAD - kw, CONV_PAD + SUBLANES)}
        phases = []
        for p in range(SUBLANES):
            xc_p = bconv_ref[:, cs] + shifted[CONV_PAD + p] * wconv_ref[kw:kw + 1, cs]
            for k in range(kw):
                xc_p = xc_p + shifted[CONV_PAD + p - kw + k] * wconv_ref[k:k + 1, cs]
            phases.append(xc_p)
        xc = jnp.concatenate(phases, axis=0)
        conv_out.append(xc)
        gate_pre.append(_dot(xc.astype(BF16), wrg_ref[g, :, 0:2 * RG_GROUP]))

    z_rest = _dot(xb, win_ref[:, 2 * D_MODEL:6 * D_MODEL])

    half_nsp = 0.5 * _neg_softplus_scaled(lam_ref[...])
    half_b = 0.5 * brg_ref[...]
    for g in range(N_RG_GROUPS):
        cs = slice(g * RG_GROUP, (g + 1) * RG_GROUP)
        a, bb = _rglru_terms(gate_pre[g], conv_out[g], half_b[:, cs], half_nsp[:, cs])
        run_a, run_h = a[0:n_grp], bb[0:n_grp]
        a_buf[0:n_grp, cs] = run_a
        b_buf[0:n_grp, cs] = run_h
        for p in range(1, SUBLANES):
            ps = slice(p * n_grp, (p + 1) * n_grp)
            run_h = a[ps] * run_h + bb[ps]
            run_a = a[ps] * run_a
            a_buf[ps, cs] = run_a
            b_buf[ps, cs] = run_h

    for c in range(N_COL_SLABS):
        xr_slab[c, 0:CONV_PAD, :] = xr_slab[c, tm:tm + CONV_PAD, :]

    last = slice((SUBLANES - 1) * n_grp, SUBLANES * n_grp)
    carry_in, h_end = _carry_rows(a_buf[last, :], b_buf[last, :], h_state[...])
    h_state[...] = h_end
    h_o_ref[...] = h_end
    for p in range(SUBLANES):
        ps = slice(p * n_grp, (p + 1) * n_grp)
        h_p = a_buf[ps, :] * carry_in + b_buf[ps, :]
        for c in range(N_COL_SLABS):
            h_slab[c, pl.ds(p, n_grp, stride=SUBLANES), :] = h_p[:, c * LANES:(c + 1) * LANES]
    h_rows = jnp.concatenate([h_slab[c] for c in range(N_COL_SLABS)], axis=1)

    _mixer_tail(x, z_rg[:, D_MODEL:2 * D_MODEL], z_rest, h_rows, mod_ref, lng_ref, lnb_ref, wtril, bsp_ref, wout_ref,
                o_ref, v_o_ref, tm, SGU_CHUNK)


def _scan_rows(a, b):
    row = lax.broadcasted_iota(jnp.int32, a.shape, 0) % SUBLANES
    for s in (1, 2, 4):
        keep = row >= s
        a_prev = jnp.where(keep, pltpu.roll(a, s, 0), 1.0)
        b_prev = jnp.where(keep, pltpu.roll(b, s, 0), 0.0)
        b = b + a * b_prev
        a = a * a_prev
    return a, b


def _mixer_short_kernel(x_ref, mod_ref, conv0_ref, h0_ref, g_ref, win_ref, wconv_ref, bconv_ref,
                        wrg_ref, brg_ref, lam_ref, lng_ref, lnb_ref, wsp_ref, bsp_ref, wout_ref,
                        o_ref, conv_o_ref, h_o_ref, v_o_ref,
                        xr_buf, a_buf, b_buf, h_buf, wtril, *, tm):
    kw = CONV_WIDTH - 1

    @pl.when(pl.program_id(0) == 0)
    def _():
        _init_tril(wtril, wsp_ref)

    x = x_ref[...]
    xb = _rms_mod(x, g_ref[...] * (1.0 + mod_ref[1]), mod_ref[0]).astype(BF16)

    xr_buf[0:CONV_PAD, :] = jnp.zeros((CONV_PAD, D_MODEL), F32)
    xr_buf[CONV_PAD - kw:CONV_PAD, :] = conv0_ref[...]
    z = _dot(xb, win_ref[:, 0:6 * D_MODEL])
    xr_buf[CONV_PAD:CONV_PAD + tm, :] = z[:, 0:D_MODEL]
    xc = bconv_ref[...] + xr_buf[CONV_PAD:CONV_PAD + tm, :] * wconv_ref[kw:kw + 1, :]
    for k in range(kw):
        off = CONV_PAD - kw + k
        xc = xc + xr_buf[off:off + tm, :] * wconv_ref[k:k + 1, :]
    conv_o_ref[...] = xr_buf[tm + CONV_PAD - kw:tm + CONV_PAD, :]

    xcb = xc.astype(BF16)
    half_nsp = 0.5 * _neg_softplus_scaled(lam_ref[...])
    half_b = 0.5 * brg_ref[...]
    for g in range(N_RG_GROUPS):
        cs = slice(g * RG_GROUP, (g + 1) * RG_GROUP)
        ri = _dot(xcb[:, cs], wrg_ref[g, :, 0:2 * RG_GROUP])
        a, bb = _rglru_terms(ri, xc[:, cs], half_b[:, cs], half_nsp[:, cs])
        a_s, b_s = _scan_rows(a, bb)
        a_buf[:, cs] = a_s
        b_buf[:, cs] = b_s

    h = h0_ref[...]
    for q in range(tm // SUBLANES):
        rs = slice(q * SUBLANES, (q + 1) * SUBLANES)
        hq = a_buf[rs, :] * h + b_buf[rs, :]
        h_buf[rs, :] = hq
        h = hq[SUBLANES - 1:SUBLANES, :]
    h_o_ref[...] = h

    _mixer_tail(x, z[:, D_MODEL:2 * D_MODEL], z[:, 2 * D_MODEL:6 * D_MODEL], h_buf[...], mod_ref, lng_ref, lnb_ref,
                wtril, bsp_ref, wout_ref, o_ref, v_o_ref, tm, tm)


def _mixer_call(x, mod, conv0, h0, weights, tm):
    b, t, _ = x.shape
    kw = CONV_WIDTH - 1
    short = t < SGU_CHUNK
    ch = t if short else SGU_CHUNK
    if short:
        assert tm == t and t % SUBLANES == 0
        grid = (b,)
        imap = lambda i: (i, 0, 0)
        imap4 = lambda i: (i, 0, 0, 0)
        body = functools.partial(_mixer_short_kernel, tm=tm)
        scratch = [
            pltpu.VMEM((tm + CONV_PAD, D_MODEL), F32),
            pltpu.VMEM((tm, D_MODEL), F32),
            pltpu.VMEM((tm, D_MODEL), F32),
            pltpu.VMEM((tm, D_MODEL), F32),
            pltpu.VMEM((SGU_HEADS, SGU_CHUNK, SGU_CHUNK), BF16),
        ]
        x_map = imap
        semantics = ("arbitrary",)
    else:
        assert t % tm == 0 and tm % SGU_CHUNK == 0
        grid = (b, t // tm)
        imap = lambda i, j: (i, 0, 0)
        imap4 = lambda i, j: (i, 0, 0, 0)
        x_map = lambda i, j: (i, j, 0)
        body = functools.partial(_mixer_kernel, tm=tm)
        scratch = [
            pltpu.VMEM((N_COL_SLABS, tm + CONV_PAD, LANES), F32),
            pltpu.VMEM((N_COL_SLABS, tm, LANES), F32),
            pltpu.VMEM((tm, D_MODEL), F32),
            pltpu.VMEM((tm, D_MODEL), F32),
            pltpu.VMEM((1, D_MODEL), F32),
            pltpu.VMEM((SGU_HEADS, SGU_CHUNK, SGU_CHUNK), BF16),
        ]
        semantics = ("arbitrary", "arbitrary")
    return pl.pallas_call(
        body,
        grid=grid,
        in_specs=[
            pl.BlockSpec((None, tm, D_MODEL), x_map),
            pl.BlockSpec((None, 3, 1, D_MODEL), imap4),
            pl.BlockSpec((None, kw, D_MODEL), imap),
            pl.BlockSpec((None, 1, D_MODEL), imap),
        ] + [_const_spec(w.shape) for w in weights],
        out_specs=[
            pl.BlockSpec((None, tm, D_MODEL), x_map),
            pl.BlockSpec((None, kw, D_MODEL), imap),
            pl.BlockSpec((None, 1, D_MODEL), imap),
            pl.BlockSpec((None, ch, D_MODEL), imap),
        ],
        out_shape=[
            jax.ShapeDtypeStruct((b, t, D_MODEL), F32),
            jax.ShapeDtypeStruct((b, kw, D_MODEL), F32),
            jax.ShapeDtypeStruct((b, 1, D_MODEL), F32),
            jax.ShapeDtypeStruct((b, ch, D_MODEL), F32),
        ],
        scratch_shapes=scratch,
        compiler_params=pltpu.CompilerParams(
            dimension_semantics=semantics,
            vmem_limit_bytes=VMEM_LIMIT_BYTES,
        ),
        name="mixer_short" if short else "mixer",
    )(x, mod, conv0, h0, *weights)


def _block_diag_gates(w_a, w_x):
    hpg = RG_GROUP // RNN_HEAD_DIM
    eye = jnp.eye(hpg, dtype=w_a.dtype)

    def bd(w):
        w = w.reshape(N_RG_GROUPS, hpg, RNN_HEAD_DIM, RNN_HEAD_DIM)
        full = w[:, :, :, None, :] * eye[None, :, None, :, None]
        return full.reshape(N_RG_GROUPS, RG_GROUP, RG_GROUP)

    return jnp.concatenate([bd(w_a), bd(w_x)], axis=-1)


def _mxu_weight(w):
    w = w.astype(BF16)
    if (w.shape[-1] // LANES) % 4 == 0:
        w = jnp.pad(w, [(0, 0)] * (w.ndim - 1) + [(0, LANES)])
    return w


def _per_row(mod, seq):
    b, k, d = mod.shape
    return jnp.broadcast_to(mod.transpose(1, 0, 2)[:, :, None, :], (k, b, seq, d)).reshape(1, k, b * seq, d)


def kernel(x_prompt, x_sample, cache_conv, state_rglru, c_prompt, c_sample, w_ada_final, b_ada_final, g_final, w_ada, b_ada, g_ffn1, w_ffn1_gate, w_ffn1_up, w_ffn1_down, g_mix, w_in, w_conv, b_conv, w_rg_a, b_rg_a, w_rg_x, b_rg_x, rg_lambda, ln_v_g, ln_v_b, w_spatial, b_spatial, w_out, g_ffn2, w_ffn2_gate, w_ffn2_up, w_ffn2_down):
    bp, tp, _ = x_prompt.shape
    bs, ts, _ = x_sample.shape
    assert w_ada.shape[0] == 1, "one trunk layer"

    c_rows = jnp.concatenate(
        [c_prompt, c_sample, jnp.zeros((ADA_ROWS - bp - bs, D_MODEL), F32)], axis=0)
    ada = _ada_call(c_rows, w_ada[0], b_ada[0]).reshape(ADA_ROWS, N_MOD, D_MODEL)
    ada_f = _ada_call(c_rows, w_ada_final, b_ada_final).reshape(ADA_ROWS, 2, D_MODEL)
    ada_p, ada_s = ada[:bp], ada[bp:bp + bs]
    adaf_p, adaf_s = ada_f[:bp], ada_f[bp:bp + bs]

    row = lambda v: v.reshape(1, D_MODEL)
    ffn1_w = (row(g_ffn1[0]), _mxu_weight(w_ffn1_gate[0]), _mxu_weight(w_ffn1_up[0]), _mxu_weight(w_ffn1_down[0]))
    ffn2_w = (row(g_ffn2[0]), _mxu_weight(w_ffn2_gate[0]), _mxu_weight(w_ffn2_up[0]), _mxu_weight(w_ffn2_down[0]))
    bias_rows = jnp.repeat(b_spatial[0].T, SGU_HEAD_DIM, axis=1)
    mix_w = (row(g_mix[0]), _mxu_weight(w_in[0]), w_conv[0], row(b_conv[0]),
             _mxu_weight(_block_diag_gates(w_rg_a[0], w_rg_x[0])), jnp.stack([b_rg_a[0], b_rg_x[0]]),
             row(rg_lambda[0]), row(ln_v_g[0]), row(ln_v_b[0]), w_spatial[0], bias_rows,
             _mxu_weight(w_out[0]))

    def layer(x, ada_b, adaf_b, conv0, h0, tm_ffn, tm_mix, flat):
        b, t, _ = x.shape
        if flat:
            xf = x.reshape(1, b * t, D_MODEL)
            m1, m3, mf = _per_row(ada_b[:, 0:3], t), _per_row(ada_b[:, 6:9], t), _per_row(adaf_b, t)
        else:
            xf = x
            m1, m3, mf = ada_b[:, 0:3, None, :], ada_b[:, 6:9, None, :], adaf_b[:, :, None, :]
        h = _ffn_call(xf, m1, *ffn1_w, tm=tm_ffn).reshape(b, t, D_MODEL)
        h, conv_n, h_n, v_n = _mixer_call(h, ada_b[:, 3:6, None, :], conv0, h0[:, None, :], mix_w, tm=tm_mix)
        y = _ffn_call(h.reshape(xf.shape), m3, *ffn2_w, tm=tm_ffn, modf=mf, gf=row(g_final))
        return y.reshape(b, t, D_MODEL), conv_n[None], h_n[:, 0][None], v_n[None]

    yp, conv_p, rg_p, v_p = layer(
        x_prompt, ada_p, adaf_p, jnp.zeros((bp, CONV_WIDTH - 1, D_MODEL), F32), jnp.zeros((bp, D_MODEL), F32),
        tm_ffn=TM_FFN, tm_mix=TM_MIX, flat=False)
    ys, conv_s, rg_s, v_s = layer(
        x_sample, ada_s, adaf_s, cache_conv[0], state_rglru[0],
        tm_ffn=bs * ts, tm_mix=ts, flat=True)
    return yp, ys, conv_p, rg_p, v_p, conv_s, rg_s, v_s
```

```python
import functools

import jax
import jax.numpy as jnp
from jax import lax
from jax.experimental import pallas as pl
from jax.experimental.pallas import tpu as pltpu

F32 = jnp.float32
BF16 = jnp.bfloat16

D_MODEL = 1024
D_FF = 2816
CONV_WIDTH = 4
RNN_HEADS = 16
RNN_HEAD_DIM = D_MODEL // RNN_HEADS
RG_C = 8.0
SGU_HEADS = 8
SGU_HEAD_DIM = D_MODEL // SGU_HEADS
SGU_CHUNK = 128
N_MOD = 9
EPS = 1e-6

SUBLANES = 8
LANES = 128
N_COL_SLABS = D_MODEL // LANES
MXU_COLS = 256
FF_COLS = MXU_COLS
ROWS = 16
RG_GROUP = 256
N_RG_GROUPS = D_MODEL // RG_GROUP
CONV_PAD = SUBLANES
ADA_ROWS = 32
ADA_COLS = 1024
TM_FFN = 512
TM_MIX = 512
VMEM_LIMIT_BYTES = 56 * 1024 * 1024
GELU_K0 = 0.7978845608028654
GELU_K1 = GELU_K0 * 0.044715


def _dot(a, b):
    return jnp.dot(a, b, preferred_element_type=F32)


def _rms_mod(x, gain, shift):
    ms = jnp.mean(x * x, axis=-1, keepdims=True)
    return (x * lax.rsqrt(ms + EPS)) * gain + shift


def _gelu_x2(x):
    return x * (1.0 + jnp.tanh(x * (GELU_K0 + GELU_K1 * (x * x))))


def _sigmoid_x2(x):
    return 1.0 + jnp.tanh(0.5 * x)


def _const_spec(shape):
    nd = len(shape)
    return pl.BlockSpec(shape, lambda *_: (0,) * nd, pipeline_mode=pl.Buffered(1))


def _ada_kernel(c_ref, w_ref, b_ref, o_ref):
    c = c_ref[...]
    sc = (c * jax.nn.sigmoid(c)).astype(BF16)
    o_ref[...] = _dot(sc, w_ref[...].astype(BF16)) + b_ref[...]


def _ada_call(c_rows, w, b):
    n = w.shape[1]
    return pl.pallas_call(
        _ada_kernel,
        grid=(n // ADA_COLS,),
        in_specs=[
            pl.BlockSpec((ADA_ROWS, D_MODEL), lambda j: (0, 0)),
            pl.BlockSpec((D_MODEL, ADA_COLS), lambda j: (0, j)),
            pl.BlockSpec((1, ADA_COLS), lambda j: (0, j)),
        ],
        out_specs=pl.BlockSpec((ADA_ROWS, ADA_COLS), lambda j: (0, j)),
        out_shape=jax.ShapeDtypeStruct((ADA_ROWS, n), F32),
        compiler_params=pltpu.CompilerParams(dimension_semantics=("parallel",)),
        name="ada_proj",
    )(c_rows, w, b.reshape(1, n))


def _ffn_kernel(x_ref, mod_ref, g_ref, wg_ref, wu_ref, wd_ref, *rest, final):
    if final:
        modf_ref, gf_ref, o_ref = rest
    else:
        (o_ref,) = rest
    x = x_ref[...]
    xb = _rms_mod(x, g_ref[...] * (1.0 + mod_ref[1]), mod_ref[0]).astype(BF16)
    acc = jnp.zeros(x.shape, F32)
    for c in range(D_FF // FF_COLS):
        cs = slice(c * FF_COLS, (c + 1) * FF_COLS)
        g = _dot(xb, wg_ref[:, cs])
        u = _dot(xb, wu_ref[:, cs])
        a = ((g * _sigmoid_x2(g)) * u).astype(BF16)
        acc = acc + _dot(a, wd_ref[cs, 0:D_MODEL])
    y = x + (0.25 * mod_ref[2]) * acc
    if final:
        y = _rms_mod(y, gf_ref[...] * (1.0 + modf_ref[1]), modf_ref[0])
    o_ref[...] = y


def _ffn_call(x, mod, g, wg, wu, wd, tm, modf=None, gf=None):
    b, t, _ = x.shape
    r = mod.shape[2]
    final = modf is not None
    in_specs = [
        pl.BlockSpec((None, tm, D_MODEL), lambda i, j: (i, j, 0)),
        pl.BlockSpec((None, 3, r, D_MODEL), lambda i, j: (i, 0, 0, 0)),
        _const_spec((1, D_MODEL)),
        _const_spec(wg.shape),
        _const_spec(wu.shape),
        _const_spec(wd.shape),
    ]
    args = [x, mod, g, wg, wu, wd]
    if final:
        in_specs += [
            pl.BlockSpec((None, 2, r, D_MODEL), lambda i, j: (i, 0, 0, 0)),
            _const_spec((1, D_MODEL)),
        ]
        args += [modf, gf]
    return pl.pallas_call(
        functools.partial(_ffn_kernel, final=final),
        grid=(b, t // tm),
        in_specs=in_specs,
        out_specs=pl.BlockSpec((None, tm, D_MODEL), lambda i, j: (i, j, 0)),
        out_shape=jax.ShapeDtypeStruct((b, t, D_MODEL), F32),
        compiler_params=pltpu.CompilerParams(
            dimension_semantics=("parallel", "parallel"),
            vmem_limit_bytes=VMEM_LIMIT_BYTES,
        ),
        name="ffn_final" if final else "ffn",
    )(*args)


def _init_tril(wtril, wsp_ref):
    row = lax.broadcasted_iota(jnp.int32, (SGU_CHUNK, SGU_CHUNK), 0)
    col = lax.broadcasted_iota(jnp.int32, (SGU_CHUNK, SGU_CHUNK), 1)
    for hh in range(SGU_HEADS):
        wtril[hh] = jnp.where(col <= row, wsp_ref[hh], 0.0).astype(BF16)


def _neg_softplus_scaled(lam):
    return -RG_C * (jnp.maximum(-lam, 0.0) + jnp.log1p(jnp.exp(-jnp.abs(lam))))


def _rglru_terms(pre_r, pre_i, xc, half_b_r, half_b_i, half_nsp):
    tr = jnp.tanh(pre_r + half_b_r)
    ti = jnp.tanh(pre_i + half_b_i)
    log_a = tr * half_nsp + half_nsp
    a = jnp.exp(log_a)
    th = jnp.tanh(log_a)
    n = -0.5 * th
    mult = jnp.where(n > 0.0, n * lax.rsqrt(n * (1.0 - th)), 0.0)
    return a, mult * ((ti + 1.0) * xc)


def _layernorm(v, g, b):
    mu = jnp.mean(v, axis=-1, keepdims=True)
    vc = v - mu
    var = jnp.mean(vc * vc, axis=-1, keepdims=True)
    return (vc * lax.rsqrt(var + EPS)) * g + b


def _spatial_gate(vnb, wtril, bias_ref, tm, ch):
    chunks = []
    for j in range(tm // ch):
        vj = vnb[j * ch:(j + 1) * ch, :]
        if ch < SGU_CHUNK:
            vj = jnp.concatenate([vj, jnp.zeros((SGU_CHUNK - ch, D_MODEL), BF16)], axis=0)
        heads = []
        for hh in range(SGU_HEADS):
            hs = slice(hh * SGU_HEAD_DIM, (hh + 1) * SGU_HEAD_DIM)
            heads.append(_dot(wtril[hh, 0:ch, :], vj[:, hs]))
        chunks.append(jnp.concatenate(heads, axis=1) + bias_ref[0:ch, :])
    return chunks[0] if len(chunks) == 1 else jnp.concatenate(chunks, axis=0)


def _mixer_tail(x, z_gate, z_rest, h_rows, mod_ref, lng_ref, lnb_ref, wtril, bsp_ref, wout_ref, o_ref, v_o_ref,
                tm, ch):
    col = lambda k: z_rest[:, k * D_MODEL:(k + 1) * D_MODEL]
    ya = h_rows * _gelu_x2(z_gate)
    u = _gelu_x2(col(0))
    vn = _layernorm(0.5 * _gelu_x2(col(1)), lng_ref[...], lnb_ref[...])
    v_o_ref[...] = vn[tm - ch:tm, :]
    yb = u * _spatial_gate(vn.astype(BF16), wtril, bsp_ref, tm, ch)
    merged = (_sigmoid_x2(col(2)) * ya + _sigmoid_x2(col(3)) * yb).astype(BF16)
    o_ref[...] = x + (0.25 * mod_ref[2]) * _dot(merged, wout_ref[:, 0:D_MODEL])


def _carry_rows(a, h, c0):
    n_rows = a.shape[0]
    sub = lax.broadcasted_iota(jnp.int32, (SUBLANES, D_MODEL), 0)
    cb = jnp.broadcast_to(c0, (SUBLANES, D_MODEL))
    out = []
    for j in range(n_rows // SUBLANES):
        aj = a[j * SUBLANES:(j + 1) * SUBLANES, :]
        hj = h[j * SUBLANES:(j + 1) * SUBLANES, :]
        cin = cb
        for s in range(SUBLANES):
            if s:
                cin = jnp.where(sub == s, cb, cin)
            nxt = aj * cb + hj
            cb = jnp.broadcast_to(nxt[s:s + 1, :], (SUBLANES, D_MODEL))
        out.append(cin)
    return jnp.concatenate(out, axis=0), cb[0:1, :]


def _mixer_kernel(x_ref, mod_ref, conv0_ref, h0_ref, g_ref, win_ref, wconv_ref, bconv_ref,
                  wrg_ref, brg_ref, lam_ref, lng_ref, lnb_ref, wsp_ref, bsp_ref, wout_ref,
                  o_ref, conv_o_ref, h_o_ref, v_o_ref,
                  xb_buf, xr_slab, zg_buf, xc_buf, xcb_buf, ri_buf, zr_buf, u_buf, a_buf, b_buf, h_slab,
                  vnb_buf, s_buf, mb_buf, h_state, wtril, *, tm):
    bi = pl.program_id(0)
    ti = pl.program_id(1)
    n_grp = tm // SUBLANES
    kw = CONV_WIDTH - 1
    row_passes = [slice(r * ROWS, (r + 1) * ROWS) for r in range(tm // ROWS)]
    phase_rows = [slice(p * n_grp, (p + 1) * n_grp) for p in range(SUBLANES)]

    @pl.when((bi == 0) & (ti == 0))
    def _():
        _init_tril(wtril, wsp_ref)

    @pl.when(ti == 0)
    def _():
        for c in range(N_COL_SLABS):
            xr_slab[c, 0:CONV_PAD, :] = jnp.zeros((CONV_PAD, LANES), F32)
            xr_slab[c, CONV_PAD - kw:CONV_PAD, :] = conv0_ref[:, c * LANES:(c + 1) * LANES]
        h_state[...] = h0_ref[...]

    gain = g_ref[...] * (1.0 + mod_ref[1])

    def norm_step(rs):
        xb_buf[rs, :] = _rms_mod(x_ref[rs, :], gain, mod_ref[0]).astype(BF16)

    def in_proj_step(c):
        res = _dot(xb_buf[...], win_ref[:, c * MXU_COLS:(c + 1) * MXU_COLS])
        k, c0 = divmod(c * MXU_COLS, D_MODEL)
        if k == 0:
            for i in range(MXU_COLS // LANES):
                xr_slab[c0 // LANES + i, CONV_PAD:CONV_PAD + tm, :] = res[:, i * LANES:(i + 1) * LANES]
        elif k == 1:
            zg_buf[:, c0:c0 + MXU_COLS] = res
        else:
            zr_buf[:, (k - 2) * D_MODEL + c0:(k - 2) * D_MODEL + c0 + MXU_COLS] = res

    def conv_step(g):
        cs = slice(g * RG_GROUP, (g + 1) * RG_GROUP)
        slabs = range(g * RG_GROUP // LANES, (g + 1) * RG_GROUP // LANES)
        shifted = {}
        for p in range(SUBLANES):
            for s in range(CONV_PAD + p - kw, CONV_PAD + p + 1):
                if s not in shifted:
                    shifted[s] = jnp.concatenate(
                        [xr_slab[c, pl.ds(s, n_grp, stride=SUBLANES), :] for c in slabs], axis=1)
            xc_p = bconv_ref[:, cs] + shifted[CONV_PAD + p] * wconv_ref[kw:kw + 1, cs]
            for k in range(kw):
                xc_p = xc_p + shifted[CONV_PAD + p - kw + k] * wconv_ref[k:k + 1, cs]
            xc_buf[phase_rows[p], cs] = xc_p
            xcb_buf[phase_rows[p], cs] = xc_p.astype(BF16)
        for c in slabs:
            conv_o_ref[:, c * LANES:(c + 1) * LANES] = xr_slab[c, tm + CONV_PAD - kw:tm + CONV_PAD, :]
            xr_slab[c, 0:CONV_PAD, :] = xr_slab[c, tm:tm + CONV_PAD, :]

    def gate_proj_step(g):
        cs = slice(g * RG_GROUP, (g + 1) * RG_GROUP)
        ri_buf[:, 2 * g * RG_GROUP:2 * (g + 1) * RG_GROUP] = _dot(xcb_buf[:, cs], wrg_ref[g, :, 0:2 * RG_GROUP])

    half_nsp = 0.5 * _neg_softplus_scaled(lam_ref[...])
    half_b = 0.5 * brg_ref[...]
    running = {}

    def scan_step(g, p):
        ps = phase_rows[p]
        cs = slice(g * RG_GROUP, (g + 1) * RG_GROUP)
        cr = slice(2 * g * RG_GROUP, (2 * g + 1) * RG_GROUP)
        ci = slice((2 * g + 1) * RG_GROUP, (2 * g + 2) * RG_GROUP)
        a, bb = _rglru_terms(ri_buf[ps, cr], ri_buf[ps, ci], xc_buf[ps, cs],
                             half_b[0:1, cs], half_b[1:2, cs], half_nsp[:, cs])
        if p:
            run_a, run_h = running[g]
            a, bb = a * run_a, a * run_h + bb
        running[g] = (a, bb)
        a_buf[ps, cs] = a
        b_buf[ps, cs] = bb

    carried = {}

    def carry_step():
        carried["in"], h_end = _carry_rows(a_buf[phase_rows[-1], :], b_buf[phase_rows[-1], :], h_state[...])
        h_state[...] = h_end
        h_o_ref[...] = h_end

    def hidden_step(p):
        h_p = a_buf[phase_rows[p], :] * carried["in"] + b_buf[phase_rows[p], :]
        for c in range(N_COL_SLABS):
            h_slab[c, pl.ds(p, n_grp, stride=SUBLANES), :] = h_p[:, c * LANES:(c + 1) * LANES]

    def sgu_in_step(r):
        rs = row_passes[r]
        u_buf[rs, :] = _gelu_x2(zr_buf[rs, 0:D_MODEL])
        vn = _layernorm(0.5 * _gelu_x2(zr_buf[rs, D_MODEL:2 * D_MODEL]), lng_ref[...], lnb_ref[...])
        vnb_buf[rs, :] = vn.astype(BF16)
        if r * ROWS >= tm - SGU_CHUNK:
            v_o_ref[r * ROWS - (tm - SGU_CHUNK):(r + 1) * ROWS - (tm - SGU_CHUNK), :] = vn

    def sgu_proj_step(j, hh):
        js = slice(j * SGU_CHUNK, (j + 1) * SGU_CHUNK)
        hs = slice(hh * SGU_HEAD_DIM, (hh + 1) * SGU_HEAD_DIM)
        s_buf[js, hs] = _dot(wtril[hh], vnb_buf[js, hs])

    def merge_step(r):
        rs = row_passes[r]
        t0 = (r * ROWS) % SGU_CHUNK
        h_rows = jnp.concatenate([h_slab[c, rs, :] for c in range(N_COL_SLABS)], axis=1)
        ya = h_rows * _gelu_x2(zg_buf[rs, :])
        yb = u_buf[rs, :] * (s_buf[rs, :] + bsp_ref[t0:t0 + ROWS, :])
        merged = (_sigmoid_x2(zr_buf[rs, 2 * D_MODEL:3 * D_MODEL]) * ya
                  + _sigmoid_x2(zr_buf[rs, 3 * D_MODEL:4 * D_MODEL]) * yb)
        mb_buf[rs, :] = merged.astype(BF16)

    gate = 0.25 * mod_ref[2]

    def out_proj_step(j, c):
        js = slice(j * SGU_CHUNK, (j + 1) * SGU_CHUNK)
        cs = slice(c * MXU_COLS, (c + 1) * MXU_COLS)
        o_ref[js, cs] = x_ref[js, cs] + gate[:, cs] * _dot(mb_buf[js, :], wout_ref[:, cs])

    def interleave(xs, ys):
        out, done = [], 0
        for i, x in enumerate(xs):
            out.append(x)
            upto = (i + 1) * len(ys) // len(xs)
            out += ys[done:upto]
            done = upto
        return out

    step = functools.partial
    n_in = D_MODEL // MXU_COLS
    passes_per_chunk = SGU_CHUNK // ROWS
    x_cols = [step(in_proj_step, c) for c in range(n_in)]
    g_cols = [step(in_proj_step, n_in + c) for c in range(n_in)]
    rest = [step(in_proj_step, 2 * n_in + c) for c in range(4 * n_in)]
    gates = [step(gate_proj_step, g) for g in range(N_RG_GROUPS)]
    phase1 = [step(norm_step, rs) for rs in row_passes]
    phase1 += [x_cols[0], x_cols[1], step(conv_step, 0), x_cols[2], step(conv_step, 1), gates[0],
               x_cols[3], step(conv_step, 2), gates[1], g_cols[0], step(conv_step, 3), gates[2],
               g_cols[1], gates[3], g_cols[2], g_cols[3]]
    phase2 = interleave(rest[:2 * n_in],
                        [step(scan_step, g, p) for g in range(N_RG_GROUPS) for p in range(SUBLANES)])
    phase3 = interleave(rest[2 * n_in:],
                        [carry_step] + [step(hidden_step, p) for p in range(SUBLANES)]
                        + [step(sgu_in_step, r) for r in range(len(row_passes))])
    n_chunks = tm // SGU_CHUNK
    sgu = [[step(sgu_proj_step, j, hh) for hh in range(SGU_HEADS)] for j in range(n_chunks)]
    merges = [[step(merge_step, j * passes_per_chunk + r) for r in range(passes_per_chunk)] for j in range(n_chunks)]
    outs = [[step(out_proj_step, j, c) for c in range(n_in)] for j in range(n_chunks)]
    phase4 = []
    for k in range(n_chunks + 2):
        mxu_steps = (sgu[k] if k < n_chunks else []) + (outs[k - 2] if 0 <= k - 2 < n_chunks else [])
        valu_steps = merges[k - 1] if 0 <= k - 1 < n_chunks else []
        phase4 += interleave(mxu_steps, valu_steps) if mxu_steps else valu_steps

    for emit in phase1 + phase2 + phase3 + phase4:
        emit()


def _scan_rows(a, b):
    row = lax.broadcasted_iota(jnp.int32, a.shape, 0) % SUBLANES
    for s in (1, 2, 4):
        keep = row >= s
        a_prev = jnp.where(keep, pltpu.roll(a, s, 0), 1.0)
        b_prev = jnp.where(keep, pltpu.roll(b, s, 0), 0.0)
        b = b + a * b_prev
        a = a * a_prev
    return a, b


def _mixer_short_kernel(x_ref, mod_ref, conv0_ref, h0_ref, g_ref, win_ref, wconv_ref, bconv_ref,
                        wrg_ref, brg_ref, lam_ref, lng_ref, lnb_ref, wsp_ref, bsp_ref, wout_ref,
                        o_ref, conv_o_ref, h_o_ref, v_o_ref,
                        xr_buf, a_buf, b_buf, h_buf, wtril, *, tm):
    kw = CONV_WIDTH - 1

    @pl.when(pl.program_id(0) == 0)
    def _():
        _init_tril(wtril, wsp_ref)

    x = x_ref[...]
    xb = _rms_mod(x, g_ref[...] * (1.0 + mod_ref[1]), mod_ref[0]).astype(BF16)

    xr_buf[0:CONV_PAD, :] = jnp.zeros((CONV_PAD, D_MODEL), F32)
    xr_buf[CONV_PAD - kw:CONV_PAD, :] = conv0_ref[...]
    z = _dot(xb, win_ref[:, 0:6 * D_MODEL])
    xr_buf[CONV_PAD:CONV_PAD + tm, :] = z[:, 0:D_MODEL]
    xc = bconv_ref[...] + xr_buf[CONV_PAD:CONV_PAD + tm, :] * wconv_ref[kw:kw + 1, :]
    for k in range(kw):
        off = CONV_PAD - kw + k
        xc = xc + xr_buf[off:off + tm, :] * wconv_ref[k:k + 1, :]
    conv_o_ref[...] = xr_buf[tm + CONV_PAD - kw:tm + CONV_PAD, :]

    xcb = xc.astype(BF16)
    half_nsp = 0.5 * _neg_softplus_scaled(lam_ref[...])
    half_b = 0.5 * brg_ref[...]
    for g in range(N_RG_GROUPS):
        cs = slice(g * RG_GROUP, (g + 1) * RG_GROUP)
        ri = _dot(xcb[:, cs], wrg_ref[g, :, 0:2 * RG_GROUP])
        a, bb = _rglru_terms(ri[:, 0:RG_GROUP], ri[:, RG_GROUP:2 * RG_GROUP], xc[:, cs],
                             half_b[0:1, cs], half_b[1:2, cs], half_nsp[:, cs])
        a_s, b_s = _scan_rows(a, bb)
        a_buf[:, cs] = a_s
        b_buf[:, cs] = b_s

    h = h0_ref[...]
    for q in range(tm // SUBLANES):
        rs = slice(q * SUBLANES, (q + 1) * SUBLANES)
        hq = a_buf[rs, :] * h + b_buf[rs, :]
        h_buf[rs, :] = hq
        h = hq[SUBLANES - 1:SUBLANES, :]
    h_o_ref[...] = h

    _mixer_tail(x, z[:, D_MODEL:2 * D_MODEL], z[:, 2 * D_MODEL:6 * D_MODEL], h_buf[...], mod_ref, lng_ref, lnb_ref,
                wtril, bsp_ref, wout_ref, o_ref, v_o_ref, tm, tm)


def _mixer_call(x, mod, conv0, h0, weights, tm):
    b, t, _ = x.shape
    kw = CONV_WIDTH - 1
    short = t < SGU_CHUNK
    ch = t if short else SGU_CHUNK
    if short:
        assert tm == t and t % SUBLANES == 0
        grid = (b,)
        imap = lambda i: (i, 0, 0)
        imap4 = lambda i: (i, 0, 0, 0)
        body = functools.partial(_mixer_short_kernel, tm=tm)
        scratch = [
            pltpu.VMEM((tm + CONV_PAD, D_MODEL), F32),
            pltpu.VMEM((tm, D_MODEL), F32),
            pltpu.VMEM((tm, D_MODEL), F32),
            pltpu.VMEM((tm, D_MODEL), F32),
            pltpu.VMEM((SGU_HEADS, SGU_CHUNK, SGU_CHUNK), BF16),
        ]
        x_map = imap
        semantics = ("arbitrary",)
    else:
        assert t % tm == 0 and tm % SGU_CHUNK == 0
        grid = (b, t // tm)
        imap = lambda i, j: (i, 0, 0)
        imap4 = lambda i, j: (i, 0, 0, 0)
        x_map = lambda i, j: (i, j, 0)
        body = functools.partial(_mixer_kernel, tm=tm)
        scratch = [
            pltpu.VMEM((tm, D_MODEL), BF16),
            pltpu.VMEM((N_COL_SLABS, tm + CONV_PAD, LANES), F32),
            pltpu.VMEM((tm, D_MODEL), F32),
            pltpu.VMEM((tm, D_MODEL), F32),
            pltpu.VMEM((tm, D_MODEL), BF16),
            pltpu.VMEM((tm, 2 * D_MODEL), F32),
            pltpu.VMEM((tm, 4 * D_MODEL), F32),
            pltpu.VMEM((tm, D_MODEL), F32),
            pltpu.VMEM((tm, D_MODEL), F32),
            pltpu.VMEM((tm, D_MODEL), F32),
            pltpu.VMEM((N_COL_SLABS, tm, LANES), F32),
            pltpu.VMEM((tm, D_MODEL), BF16),
            pltpu.VMEM((tm, D_MODEL), F32),
            pltpu.VMEM((tm, D_MODEL), BF16),
            pltpu.VMEM((1, D_MODEL), F32),
            pltpu.VMEM((SGU_HEADS, SGU_CHUNK, SGU_CHUNK), BF16),
        ]
        semantics = ("arbitrary", "arbitrary")
    return pl.pallas_call(
        body,
        grid=grid,
        in_specs=[
            pl.BlockSpec((None, tm, D_MODEL), x_map),
            pl.BlockSpec((None, 3, 1, D_MODEL), imap4),
            pl.BlockSpec((None, kw, D_MODEL), imap),
            pl.BlockSpec((None, 1, D_MODEL), imap),
        ] + [_const_spec(w.shape) for w in weights],
        out_specs=[
            pl.BlockSpec((None, tm, D_MODEL), x_map),
            pl.BlockSpec((None, kw, D_MODEL), imap),
            pl.BlockSpec((None, 1, D_MODEL), imap),
            pl.BlockSpec((None, ch, D_MODEL), imap),
        ],
        out_shape=[
            jax.ShapeDtypeStruct((b, t, D_MODEL), F32),
            jax.ShapeDtypeStruct((b, kw, D_MODEL), F32),
            jax.ShapeDtypeStruct((b, 1, D_MODEL), F32),
            jax.ShapeDtypeStruct((b, ch, D_MODEL), F32),
        ],
        scratch_shapes=scratch,
        compiler_params=pltpu.CompilerParams(
            dimension_semantics=semantics,
            vmem_limit_bytes=VMEM_LIMIT_BYTES,
        ),
        name="mixer_short" if short else "mixer",
    )(x, mod, conv0, h0, *weights)


def _block_diag_gates(w_a, w_x):
    hpg = RG_GROUP // RNN_HEAD_DIM
    eye = jnp.eye(hpg, dtype=w_a.dtype)

    def bd(w):
        w = w.reshape(N_RG_GROUPS, hpg, RNN_HEAD_DIM, RNN_HEAD_DIM)
        full = w[:, :, :, None, :] * eye[None, :, None, :, None]
        return full.reshape(N_RG_GROUPS, RG_GROUP, RG_GROUP)

    return jnp.concatenate([bd(w_a), bd(w_x)], axis=-1)


def _mxu_weight(w):
    w = w.astype(BF16)
    if (w.shape[-1] // LANES) % 4 == 0:
        w = jnp.pad(w, [(0, 0)] * (w.ndim - 1) + [(0, LANES)])
    return w


def _per_row(mod, seq):
    b, k, d = mod.shape
    return jnp.broadcast_to(mod.transpose(1, 0, 2)[:, :, None, :], (k, b, seq, d)).reshape(1, k, b * seq, d)


def kernel(x_prompt, x_sample, cache_conv, state_rglru, c_prompt, c_sample, w_ada_final, b_ada_final, g_final, w_ada, b_ada, g_ffn1, w_ffn1_gate, w_ffn1_up, w_ffn1_down, g_mix, w_in, w_conv, b_conv, w_rg_a, b_rg_a, w_rg_x, b_rg_x, rg_lambda, ln_v_g, ln_v_b, w_spatial, b_spatial, w_out, g_ffn2, w_ffn2_gate, w_ffn2_up, w_ffn2_down):
    bp, tp, _ = x_prompt.shape
    bs, ts, _ = x_sample.shape
    assert w_ada.shape[0] == 1, "one trunk layer"

    c_rows = jnp.concatenate(
        [c_prompt, c_sample, jnp.zeros((ADA_ROWS - bp - bs, D_MODEL), F32)], axis=0)
    ada = _ada_call(c_rows, w_ada[0], b_ada[0]).reshape(ADA_ROWS, N_MOD, D_MODEL)
    ada_f = _ada_call(c_rows, w_ada_final, b_ada_final).reshape(ADA_ROWS, 2, D_MODEL)
    ada_p, ada_s = ada[:bp], ada[bp:bp + bs]
    adaf_p, adaf_s = ada_f[:bp], ada_f[bp:bp + bs]

    row = lambda v: v.reshape(1, D_MODEL)
    ffn1_w = (row(g_ffn1[0]), _mxu_weight(w_ffn1_gate[0]), _mxu_weight(w_ffn1_up[0]), _mxu_weight(w_ffn1_down[0]))
    ffn2_w = (row(g_ffn2[0]), _mxu_weight(w_ffn2_gate[0]), _mxu_weight(w_ffn2_up[0]), _mxu_weight(w_ffn2_down[0]))
    bias_rows = jnp.repeat(b_spatial[0].T, SGU_HEAD_DIM, axis=1)
    mix_w = (row(g_mix[0]), _mxu_weight(w_in[0]), w_conv[0], row(b_conv[0]),
             _mxu_weight(0.5 * _block_diag_gates(w_rg_a[0], w_rg_x[0])), jnp.stack([b_rg_a[0], b_rg_x[0]]),
             row(rg_lambda[0]), row(ln_v_g[0]), row(ln_v_b[0]), w_spatial[0], bias_rows,
             _mxu_weight(w_out[0]))

    def layer(x, ada_b, adaf_b, conv0, h0, tm_ffn, tm_mix, flat):
        b, t, _ = x.shape
        if flat:
            xf = x.reshape(1, b * t, D_MODEL)
            m1, m3, mf = _per_row(ada_b[:, 0:3], t), _per_row(ada_b[:, 6:9], t), _per_row(adaf_b, t)
        else:
            xf = x
            m1, m3, mf = ada_b[:, 0:3, None, :], ada_b[:, 6:9, None, :], adaf_b[:, :, None, :]
        h = _ffn_call(xf, m1, *ffn1_w, tm=tm_ffn).reshape(b, t, D_MODEL)
        h, conv_n, h_n, v_n = _mixer_call(h, ada_b[:, 3:6, None, :], conv0, h0[:, None, :], mix_w, tm=tm_mix)
        y = _ffn_call(h.reshape(xf.shape), m3, *ffn2_w, tm=tm_ffn, modf=mf, gf=row(g_final))
        return y.reshape(b, t, D_MODEL), conv_n[None], h_n[:, 0][None], v_n[None]

    yp, conv_p, rg_p, v_p = layer(
        x_prompt, ada_p, adaf_p, jnp.zeros((bp, CONV_WIDTH - 1, D_MODEL), F32), jnp.zeros((bp, D_MODEL), F32),
        tm_ffn=TM_FFN, tm_mix=TM_MIX, flat=False)
    ys, conv_s, rg_s, v_s = layer(
        x_sample, ada_s, adaf_s, cache_conv[0], state_rglru[0],
        tm_ffn=bs * ts, tm_mix=ts, flat=True)
    return yp, ys, conv_p, rg_p, v_p, conv_s, rg_s, v_s
```
